```python
import jax, jax.numpy as jnp
from jax import lax
import numpy as np


D_MODEL = 1024
BATCH = 4
SEQ = 4096
DEPTH = 2

CONV_CH = D_MODEL
CONV_K = 31
SG_GROUPS = 8
SG_CH = D_MODEL
SG_DG = SG_CH // SG_GROUPS
CHUNK = 128
N_IN = 2 * CONV_CH + 2 * SG_CH + 2 * D_MODEL
FF_DENSE = 2816
N_EXPERTS = 8
TOP_K = 2
FF_EXPERT = 3584
N_DENSE = (DEPTH + 1) // 2
N_MOE = DEPTH // 2
EPS = 1e-6

kernel_name = "hybrid_conv_gmlp_gated_moe_block"


def rmsnorm(x, g):
    xf = x.astype(jnp.float32)
    y = xf * lax.rsqrt(jnp.mean(xf * xf, axis=-1, keepdims=True) + EPS)
    return (y * g.astype(jnp.float32)).astype(x.dtype)


def layernorm(x, g, b):
    xf = x.astype(jnp.float32)
    mu = jnp.mean(xf, axis=-1, keepdims=True)
    var = jnp.mean(jnp.square(xf - mu), axis=-1, keepdims=True)
    y = (xf - mu) * lax.rsqrt(var + EPS)
    return (y * g.astype(jnp.float32) + b.astype(jnp.float32)).astype(x.dtype)


def conv_branch(a_val, a_gate, conv_w, conv_b, ln_g, ln_b, w_out):
    a = a_val * jax.nn.sigmoid(a_gate)
    a = lax.conv_general_dilated(
        a, conv_w[:, None, :], window_strides=(1,),
        padding=[(CONV_K - 1, 0)],
        dimension_numbers=("NWC", "WIO", "NWC"),
        feature_group_count=CONV_CH) + conv_b
    a = jax.nn.silu(layernorm(a, ln_g, ln_b))
    return a @ w_out


def sgu_branch(u, v, ln_g, ln_b, w_s, b_s, w_out):
    bsz, seq, _ = u.shape
    u = jax.nn.gelu(u, approximate=False)
    v = jax.nn.gelu(v, approximate=False)
    v = v.reshape(bsz, seq // CHUNK, CHUNK, SG_GROUPS, SG_DG)
    v = layernorm(v, ln_g.reshape(SG_GROUPS, SG_DG), ln_b.reshape(SG_GROUPS, SG_DG))
    mask = jnp.tril(jnp.ones((CHUNK, CHUNK), dtype=bool))
    w = jnp.where(mask[None], w_s, jnp.zeros_like(w_s))
    v = jnp.einsum('gts,bnsgd->bntgd', w, v) + b_s.T[:, :, None]
    y = u * v.reshape(bsz, seq, SG_CH)
    return y @ w_out


def swiglu(h, w1, w3, w2):
    return (jax.nn.silu(h @ w1) * (h @ w3)) @ w2


def moe_ffn(h, router, w1, w3, w2):
    bsz, seq, d = h.shape
    t = h.reshape(bsz * seq, d)
    logits = t.astype(jnp.float32) @ router.astype(jnp.float32)
    top_vals, top_idx = lax.top_k(logits, TOP_K)
    top_w = jax.nn.softmax(top_vals, axis=-1)
    combine = jnp.sum(jax.nn.one_hot(top_idx, N_EXPERTS, dtype=jnp.float32)
                      * top_w[..., None], axis=1).astype(h.dtype)
    out = jnp.zeros_like(t)
    for e in range(N_EXPERTS):
        out = out + combine[:, e:e + 1] * swiglu(t, w1[e], w3[e], w2[e])
    return out.reshape(bsz, seq, d)


def setup_inputs(seed: int = 0) -> dict:
    key = jax.random.key(seed)
    ks = jax.random.split(key, 32)
    f32 = jnp.float32
    D = D_MODEL

    def nrm(k, shape, scale):
        return jax.random.normal(k, shape, f32) * scale

    def gain(k, shape):
        return 1.0 + 0.01 * jax.random.normal(k, shape, f32)

    return {
        "x": nrm(ks[0], (BATCH, SEQ, D), 1.0),
        "g_mix": gain(ks[1], (DEPTH, D)),
        "w_in": nrm(ks[2], (DEPTH, D, N_IN), D ** -0.5),
        "conv_w": nrm(ks[3], (DEPTH, CONV_K, CONV_CH), CONV_K ** -0.5),
        "conv_b": nrm(ks[4], (DEPTH, CONV_CH), 0.01),
        "conv_ln_g": gain(ks[5], (DEPTH, CONV_CH)),
        "conv_ln_b": nrm(ks[6], (DEPTH, CONV_CH), 0.01),
        "w_conv_out": nrm(ks[7], (DEPTH, CONV_CH, D), CONV_CH ** -0.5),
        "sg_ln_g": gain(ks[8], (DEPTH, SG_CH)),
        "sg_ln_b": nrm(ks[9], (DEPTH, SG_CH), 0.01),
        "sg_w": nrm(ks[10], (DEPTH, SG_GROUPS, CHUNK, CHUNK), CHUNK ** -0.5),
        "sg_b": gain(ks[11], (DEPTH, SG_GROUPS, CHUNK)),
        "w_sg_out": nrm(ks[12], (DEPTH, SG_CH, D), SG_CH ** -0.5),
        "w_o": nrm(ks[13], (DEPTH, D, D), D ** -0.5),
        "g_ffn": gain(ks[14], (DEPTH, D)),
        "ffn_w1": nrm(ks[15], (N_DENSE, D, FF_DENSE), D ** -0.5),
        "ffn_w3": nrm(ks[16], (N_DENSE, D, FF_DENSE), D ** -0.5),
        "ffn_w2": nrm(ks[17], (N_DENSE, FF_DENSE, D), FF_DENSE ** -0.5),
        "moe_router": nrm(ks[18], (N_MOE, D, N_EXPERTS), D ** -0.5),
        "moe_w1": nrm(ks[19], (N_MOE, N_EXPERTS, D, FF_EXPERT), D ** -0.5),
        "moe_w3": nrm(ks[20], (N_MOE, N_EXPERTS, D, FF_EXPERT), D ** -0.5),
        "moe_w2": nrm(ks[21], (N_MOE, N_EXPERTS, FF_EXPERT, D), FF_EXPERT ** -0.5),
        "g_final": gain(ks[22], (D,)),
    }


def reference(x, g_mix, w_in, conv_w, conv_b, conv_ln_g, conv_ln_b, w_conv_out,
              sg_ln_g, sg_ln_b, sg_w, sg_b, w_sg_out, w_o, g_ffn,
              ffn_w1, ffn_w3, ffn_w2, moe_router, moe_w1, moe_w3, moe_w2, g_final):
    o_ag = CONV_CH
    o_u = 2 * CONV_CH
    o_v = o_u + SG_CH
    o_ga = o_v + SG_CH
    o_gb = o_ga + D_MODEL
    for l in range(DEPTH):
        h = rmsnorm(x, g_mix[l])
        p = h @ w_in[l]
        ya = conv_branch(p[..., :o_ag], p[..., o_ag:o_u], conv_w[l], conv_b[l],
                         conv_ln_g[l], conv_ln_b[l], w_conv_out[l])
        yb = sgu_branch(p[..., o_u:o_v], p[..., o_v:o_ga], sg_ln_g[l], sg_ln_b[l],
                        sg_w[l], sg_b[l], w_sg_out[l])
        m = jax.nn.sigmoid(p[..., o_ga:o_gb]) * ya + jax.nn.sigmoid(p[..., o_gb:]) * yb
        x = x + m @ w_o[l]
        h = rmsnorm(x, g_ffn[l])
        j = l // 2
        if l % 2 == 0:
            x = x + swiglu(h, ffn_w1[j], ffn_w3[j], ffn_w2[j])
        else:
            x = x + moe_ffn(h, moe_router[j], moe_w1[j], moe_w3[j], moe_w2[j])
    return rmsnorm(x, g_final)
```

```python
import functools

import jax
import jax.numpy as jnp
import numpy as np
from jax import lax
from jax.experimental import pallas as pl
from jax.experimental.pallas import tpu as pltpu

F32 = jnp.float32
BF16 = jnp.bfloat16

D = 1024
SEQ = 4096
DEPTH = 2
CONV_K = 31
GROUPS = 8
CHUNK = 128
DG = 128
N_IN = 6 * D
FF_DENSE = 2816
N_EXPERTS = 8
FF_EXPERT = 3584
EPS = 1e-6

LANES = 128
HALO = 32
TM = 512
RB = 32
CRB = 64
VMEM_LIMIT = 56 * 1024 * 1024

SQRT_HALF = float(np.sqrt(0.5))


def _gelu(x):
    return 0.5 * x * (1.0 + lax.erf(x * SQRT_HALF))


def _resident(shape):
    nd = len(shape)
    return pl.BlockSpec(shape, lambda i: (0,) * nd, pipeline_mode=pl.Buffered(1))


def _rows(r, n):
    return pl.ds(pl.multiple_of(r * n, n), n)


def _mixer_kernel(x_ref, gmix_ref, win_ref, cw_ref, cb_ref, clg_ref, clb_ref, wco_ref,
                  slg_ref, slb_ref, sgw_ref, sgbt_ref, wso_ref, wo_ref,
                  o_ref,
                  h_scr, p_scr, a_scr, c_scr, s_scr, y_scr, m_scr, wm_scr):
    i = pl.program_id(0)

    def rms_body(r, carry):
        rows = _rows(r, RB)
        xb = x_ref[rows, :]
        ms = jnp.mean(xb * xb, axis=-1, keepdims=True)
        h_scr[rows, :] = (xb * lax.rsqrt(ms + EPS) * gmix_ref[...]).astype(BF16)
        return carry
    lax.fori_loop(0, TM // RB, rms_body, 0)

    p_scr[...] = jnp.dot(h_scr[...], win_ref[...], preferred_element_type=F32)

    @pl.when(i % (SEQ // TM) == 0)
    def _():
        a_scr[0:HALO, :] = jnp.zeros((HALO, D), F32)

    @pl.when(i % (SEQ // TM) != 0)
    def _():
        a_scr[0:HALO, :] = a_scr[TM:TM + HALO, :]

    def glu_body(r, carry):
        rows = _rows(r, RB)
        a_scr[pl.ds(pl.multiple_of(HALO + r * RB, RB), RB), :] = (
            p_scr[rows, 0:D] * jax.nn.sigmoid(p_scr[rows, D:2 * D]))
        return carry
    lax.fori_loop(0, TM // RB, glu_body, 0)

    off = HALO - (CONV_K - 1)

    def conv_body(cb, carry):
        cols = pl.ds(pl.multiple_of(cb * LANES, LANES), LANES)
        w = cw_ref[:, cols]
        for rb in range(TM // CRB):
            acc = jnp.zeros((CRB, LANES), F32)
            for k in range(CONV_K):
                lo = rb * CRB + k + off
                acc = acc + w[k:k + 1, :] * a_scr[lo:lo + CRB, cols]
            c_scr[rb * CRB:(rb + 1) * CRB, cols] = acc + cb_ref[:, cols]
        return carry
    lax.fori_loop(0, D // LANES, conv_body, 0)

    def ln_body(r, carry):
        rows = _rows(r, RB)
        c = c_scr[rows, :]
        mu = jnp.mean(c, axis=-1, keepdims=True)
        d = c - mu
        var = jnp.mean(d * d, axis=-1, keepdims=True)
        y = d * lax.rsqrt(var + EPS) * clg_ref[...] + clb_ref[...]
        s_scr[rows, :] = (y * jax.nn.sigmoid(y)).astype(BF16)
        return carry
    lax.fori_loop(0, TM // RB, ln_body, 0)

    m_scr[...] = jax.nn.sigmoid(p_scr[:, 4 * D:5 * D]) * jnp.dot(
        s_scr[...], wco_ref[...], preferred_element_type=F32)

    tr = lax.broadcasted_iota(jnp.int32, (CHUNK, CHUNK), 0)
    tc = lax.broadcasted_iota(jnp.int32, (CHUNK, CHUNK), 1)
    for g in range(GROUPS):
        wm_scr[g] = jnp.where(tr >= tc, sgw_ref[g], 0.0).astype(BF16)

    def sgu_body(c, carry):
        rows = _rows(c, CHUNK)
        for g in range(GROUPS):
            lo = g * DG
            v = _gelu(p_scr[rows, 3 * D + lo:3 * D + lo + DG])
            mu = jnp.mean(v, axis=-1, keepdims=True)
            d = v - mu
            var = jnp.mean(d * d, axis=-1, keepdims=True)
            vn = d * lax.rsqrt(var + EPS) * slg_ref[:, lo:lo + DG] + slb_ref[:, lo:lo + DG]
            mixed = jnp.dot(wm_scr[g], vn.astype(BF16), preferred_element_type=F32)
            mixed = mixed + sgbt_ref[:, g:g + 1]
            u = _gelu(p_scr[rows, 2 * D + lo:2 * D + lo + DG])
            y_scr[rows, lo:lo + DG] = (u * mixed).astype(BF16)
        return carry
    lax.fori_loop(0, TM // CHUNK, sgu_body, 0)

    yb = jnp.dot(y_scr[...], wso_ref[...], preferred_element_type=F32)
    m = m_scr[...] + jax.nn.sigmoid(p_scr[:, 5 * D:6 * D]) * yb
    o_ref[...] = x_ref[...] + jnp.dot(m.astype(BF16), wo_ref[...], preferred_element_type=F32)


def _mixer(x, gmix, win, cw, cb, clg, clb, wco, slg, slb, sgw, sgbt, wso, wo):
    t = x.shape[0]
    tile = pl.BlockSpec((TM, D), lambda i: (i, 0))
    return pl.pallas_call(
        _mixer_kernel,
        grid=(t // TM,),
        in_specs=[tile, _resident((1, D)), _resident((D, N_IN)), _resident((HALO, D)),
                  _resident((1, D)), _resident((1, D)), _resident((1, D)), _resident((D, D)),
                  _resident((1, D)), _resident((1, D)), _resident((GROUPS, CHUNK, CHUNK)),
                  _resident((CHUNK, GROUPS)), _resident((D, D)), _resident((D, D))],
        out_specs=tile,
        out_shape=jax.ShapeDtypeStruct((t, D), F32),
        scratch_shapes=[
            pltpu.VMEM((TM, D), BF16),
            pltpu.VMEM((TM, N_IN), F32),
            pltpu.VMEM((TM + HALO, D), F32),
            pltpu.VMEM((TM, D), F32),
            pltpu.VMEM((TM, D), BF16),
            pltpu.VMEM((TM, D), BF16),
            pltpu.VMEM((TM, D), F32),
            pltpu.VMEM((GROUPS, CHUNK, CHUNK), BF16),
        ],
        compiler_params=pltpu.CompilerParams(
            dimension_semantics=("arbitrary",), vmem_limit_bytes=VMEM_LIMIT),
        name="mixer",
    )(x, gmix, win, cw, cb, clg, clb, wco, slg, slb, sgw, sgbt, wso, wo)


FC_DENSE = 1408


def _ffn_kernel(x_ref, g_ref, w1_ref, w3_ref, w2_ref, o_ref, h_scr):
    def rms_body(r, carry):
        rows = _rows(r, RB)
        xb = x_ref[rows, :]
        ms = jnp.mean(xb * xb, axis=-1, keepdims=True)
        h_scr[rows, :] = (xb * lax.rsqrt(ms + EPS) * g_ref[...]).astype(BF16)
        return carry
    lax.fori_loop(0, TM // RB, rms_body, 0)

    h = h_scr[...]
    acc = x_ref[...]
    for c in range(FF_DENSE // FC_DENSE):
        lo = c * FC_DENSE
        h1 = jnp.dot(h, w1_ref[:, lo:lo + FC_DENSE], preferred_element_type=F32)
        h3 = jnp.dot(h, w3_ref[:, lo:lo + FC_DENSE], preferred_element_type=F32)
        gate = (h1 * jax.nn.sigmoid(h1) * h3).astype(BF16)
        acc = acc + jnp.dot(gate, w2_ref[lo:lo + FC_DENSE, :], preferred_element_type=F32)
    o_ref[...] = acc


def _ffn(x, g, w1, w3, w2):
    t = x.shape[0]
    tile = pl.BlockSpec((TM, D), lambda i: (i, 0))
    return pl.pallas_call(
        _ffn_kernel,
        grid=(t // TM,),
        in_specs=[tile, _resident((1, D)), _resident((D, FF_DENSE)), _resident((D, FF_DENSE)),
                  _resident((FF_DENSE, D))],
        out_specs=tile,
        out_shape=jax.ShapeDtypeStruct((t, D), F32),
        scratch_shapes=[pltpu.VMEM((TM, D), BF16)],
        compiler_params=pltpu.CompilerParams(
            dimension_semantics=("arbitrary",), vmem_limit_bytes=VMEM_LIMIT),
        name="ffn_dense",
    )(x, g, w1, w3, w2)


def _router_kernel(x_ref, g_ref, r_ref, h_ref, comb_ref):
    x = x_ref[...]
    ms = jnp.mean(x * x, axis=-1, keepdims=True)
    h = x * lax.rsqrt(ms + EPS) * g_ref[...]
    h_ref[...] = h.astype(BF16)
    logits = jnp.dot(h, r_ref[...], preferred_element_type=F32, precision=lax.Precision.HIGHEST)
    lane = lax.broadcasted_iota(jnp.int32, logits.shape, 1)
    neg = jnp.float32(-jnp.inf)
    logits = jnp.where(lane < N_EXPERTS, logits, neg)
    m1 = jnp.max(logits, axis=-1, keepdims=True)
    i1 = jnp.min(jnp.where(logits == m1, lane, LANES), axis=-1, keepdims=True)
    rest = jnp.where(lane == i1, neg, logits)
    m2 = jnp.max(rest, axis=-1, keepdims=True)
    i2 = jnp.min(jnp.where(rest == m2, lane, LANES), axis=-1, keepdims=True)
    e2 = jnp.exp(m2 - m1)
    den = 1.0 + e2
    comb_ref[...] = jnp.where(lane == i1, 1.0 / den, 0.0) + jnp.where(lane == i2, e2 / den, 0.0)


RT = 256


def _router(x, g, r_pad):
    t = x.shape[0]
    return pl.pallas_call(
        _router_kernel,
        grid=(t // RT,),
        in_specs=[pl.BlockSpec((RT, D), lambda i: (i, 0)), _resident((1, D)), _resident((D, LANES))],
        out_specs=[pl.BlockSpec((RT, D), lambda i: (i, 0)), pl.BlockSpec((RT, LANES), lambda i: (i, 0))],
        out_shape=[jax.ShapeDtypeStruct((t, D), BF16), jax.ShapeDtypeStruct((t, LANES), F32)],
        compiler_params=pltpu.CompilerParams(
            dimension_semantics=("arbitrary",), vmem_limit_bytes=VMEM_LIMIT),
        name="router",
    )(x, g, r_pad)


FC_MOE = 512
NFC = FF_EXPERT // FC_MOE
TM_MOE = 1024


def _moe_dense_kernel(x_ref, h_ref, comb_ref, w1_ref, w3_ref, w2_ref, o_ref, acc_scr):
    j = pl.program_id(1)
    e = j // NFC

    @pl.when(j == 0)
    def _():
        acc_scr[...] = x_ref[...]

    h = h_ref[...]
    h1 = jnp.dot(h, w1_ref[0], preferred_element_type=F32)
    h3 = jnp.dot(h, w3_ref[0], preferred_element_type=F32)
    gate = (h1 * jax.nn.sigmoid(h1) * h3).astype(BF16)
    y = jnp.dot(gate, w2_ref[0], preferred_element_type=F32)
    comb = comb_ref[...]
    lane = lax.broadcasted_iota(jnp.int32, comb.shape, 1)
    col = jnp.sum(jnp.where(lane == e, comb, 0.0), axis=-1, keepdims=True)
    acc_scr[...] += col * y

    @pl.when(j == pl.num_programs(1) - 1)
    def _():
        o_ref[...] = acc_scr[...]


def _moe_dense(x, h, comb, w1, w3, w2):
    t = x.shape[0]
    return pl.pallas_call(
        _moe_dense_kernel,
        grid=(t // TM_MOE, N_EXPERTS * NFC),
        in_specs=[pl.BlockSpec((TM_MOE, D), lambda i, j: (i, 0)),
                  pl.BlockSpec((TM_MOE, D), lambda i, j: (i, 0)),
                  pl.BlockSpec((TM_MOE, LANES), lambda i, j: (i, 0)),
                  pl.BlockSpec((1, D, FC_MOE), lambda i, j: (j // NFC, 0, j % NFC)),
                  pl.BlockSpec((1, D, FC_MOE), lambda i, j: (j // NFC, 0, j % NFC)),
                  pl.BlockSpec((1, FC_MOE, D), lambda i, j: (j // NFC, j % NFC, 0))],
        out_specs=pl.BlockSpec((TM_MOE, D), lambda i, j: (i, 0)),
        out_shape=jax.ShapeDtypeStruct((t, D), F32),
        scratch_shapes=[pltpu.VMEM((TM_MOE, D), F32)],
        compiler_params=pltpu.CompilerParams(
            dimension_semantics=("arbitrary", "arbitrary"), vmem_limit_bytes=VMEM_LIMIT),
        name="moe_dense",
    )(x, h, comb, w1, w3, w2)


def _final_norm_kernel(x_ref, g_ref, o_ref):
    x = x_ref[...]
    ms = jnp.mean(x * x, axis=-1, keepdims=True)
    o_ref[...] = x * lax.rsqrt(ms + EPS) * g_ref[...]


def _final_norm(x, g):
    t = x.shape[0]
    tile = pl.BlockSpec((TM, D), lambda i: (i, 0))
    return pl.pallas_call(
        _final_norm_kernel,
        grid=(t // TM,),
        in_specs=[tile, _resident((1, D))],
        out_specs=tile,
        out_shape=jax.ShapeDtypeStruct((t, D), F32),
        compiler_params=pltpu.CompilerParams(dimension_semantics=("arbitrary",)),
        name="final_norm",
    )(x, g)


def kernel(x, g_mix, w_in, conv_w, conv_b, conv_ln_g, conv_ln_b, w_conv_out, sg_ln_g, sg_ln_b,
           sg_w, sg_b, w_sg_out, w_o, g_ffn, ffn_w1, ffn_w3, ffn_w2, moe_router, moe_w1, moe_w3,
           moe_w2, g_final):
    bsz, seq, d = x.shape
    assert (seq, d) == (SEQ, D) and SEQ % TM == 0
    xt = x.reshape(bsz * seq, d)
    row = lambda v: v.reshape(1, -1)
    for l in range(DEPTH):
        cw = jnp.pad(conv_w[l], ((0, HALO - CONV_K), (0, 0)))
        xt = _mixer(xt, row(g_mix[l]), w_in[l].astype(BF16), cw, row(conv_b[l]),
                    row(conv_ln_g[l]), row(conv_ln_b[l]), w_conv_out[l].astype(BF16),
                    row(sg_ln_g[l]), row(sg_ln_b[l]), sg_w[l], sg_b[l].T,
                    w_sg_out[l].astype(BF16), w_o[l].astype(BF16))
        j = l // 2
        if l % 2 == 0:
            xt = _ffn(xt, row(g_ffn[l]), ffn_w1[j].astype(BF16), ffn_w3[j].astype(BF16),
                      ffn_w2[j].astype(BF16))
        else:
            r_pad = jnp.pad(moe_router[j], ((0, 0), (0, LANES - N_EXPERTS)))
            h, comb = _router(xt, row(g_ffn[l]), r_pad)
            xt = _moe_dense(xt, h, comb, moe_w1[j].astype(BF16), moe_w3[j].astype(BF16),
                            moe_w2[j].astype(BF16))
    out = _final_norm(xt, row(g_final))
    return out.reshape(bsz, seq, d)
```

```python
import functools

import jax
import jax.numpy as jnp
import numpy as np
from jax import lax
from jax.experimental import pallas as pl
from jax.experimental.pallas import tpu as pltpu

F32 = jnp.float32
BF16 = jnp.bfloat16

D = 1024
SEQ = 4096
DEPTH = 2
CONV_K = 31
GROUPS = 8
CHUNK = 128
DG = 128
N_IN = 6 * D
FF_DENSE = 2816
N_EXPERTS = 8
FF_EXPERT = 3584
EPS = 1e-6

LANES = 128
HALO = 32
TM = 512
RB = 32
CRB = 64
VMEM_LIMIT = 56 * 1024 * 1024

SQRT_HALF = float(np.sqrt(0.5))


def _gelu(x):
    return 0.5 * x * (1.0 + lax.erf(x * SQRT_HALF))


def _resident(shape):
    nd = len(shape)
    return pl.BlockSpec(shape, lambda i: (0,) * nd, pipeline_mode=pl.Buffered(1))


def _rows(r, n):
    return pl.ds(pl.multiple_of(r * n, n), n)


def _mixer_kernel(x_ref, gmix_ref, win_ref, cw_ref, cb_ref, clg_ref, clb_ref, wco_ref,
                  slg_ref, slb_ref, sgw_ref, sgbt_ref, wso_ref, wo_ref,
                  o_ref,
                  h_scr, p_scr, a_scr, c_scr, s_scr, y_scr, m_scr, wm_scr):
    i = pl.program_id(0)

    def rms_body(r, carry):
        rows = _rows(r, RB)
        xb = x_ref[rows, :]
        ms = jnp.mean(xb * xb, axis=-1, keepdims=True)
        h_scr[rows, :] = (xb * lax.rsqrt(ms + EPS) * gmix_ref[...]).astype(BF16)
        return carry
    lax.fori_loop(0, TM // RB, rms_body, 0)

    p_scr[...] = jnp.dot(h_scr[...], win_ref[...], preferred_element_type=F32)

    @pl.when(i % (SEQ // TM) == 0)
    def _():
        a_scr[0:HALO, :] = jnp.zeros((HALO, D), F32)

    @pl.when(i % (SEQ // TM) != 0)
    def _():
        a_scr[0:HALO, :] = a_scr[TM:TM + HALO, :]

    def glu_body(r, carry):
        rows = _rows(r, RB)
        a_scr[pl.ds(pl.multiple_of(HALO + r * RB, RB), RB), :] = (
            p_scr[rows, 0:D] * jax.nn.sigmoid(p_scr[rows, D:2 * D]))
        return carry
    lax.fori_loop(0, TM // RB, glu_body, 0)

    off = HALO - (CONV_K - 1)

    def conv_body(cb, carry):
        cols = pl.ds(pl.multiple_of(cb * LANES, LANES), LANES)
        w = cw_ref[:, cols]
        for rb in range(TM // CRB):
            acc = jnp.zeros((CRB, LANES), F32)
            for k in range(CONV_K):
                lo = rb * CRB + k + off
                acc = acc + w[k:k + 1, :] * a_scr[lo:lo + CRB, cols]
            c_scr[rb * CRB:(rb + 1) * CRB, cols] = acc + cb_ref[:, cols]
        return carry
    lax.fori_loop(0, D // LANES, conv_body, 0)

    def ln_body(r, carry):
        rows = _rows(r, RB)
        c = c_scr[rows, :]
        mu = jnp.mean(c, axis=-1, keepdims=True)
        d = c - mu
        var = jnp.mean(d * d, axis=-1, keepdims=True)
        y = d * lax.rsqrt(var + EPS) * clg_ref[...] + clb_ref[...]
        s_scr[rows, :] = (y * jax.nn.sigmoid(y)).astype(BF16)
        return carry
    lax.fori_loop(0, TM // RB, ln_body, 0)

    m_scr[...] = jax.nn.sigmoid(p_scr[:, 4 * D:5 * D]) * jnp.dot(
        s_scr[...], wco_ref[...], preferred_element_type=F32)

    tr = lax.broadcasted_iota(jnp.int32, (CHUNK, CHUNK), 0)
    tc = lax.broadcasted_iota(jnp.int32, (CHUNK, CHUNK), 1)
    for g in range(GROUPS):
        wm_scr[g] = jnp.where(tr >= tc, sgw_ref[g], 0.0).astype(BF16)

    def sgu_body(c, carry):
        rows = _rows(c, CHUNK)
        for g in range(GROUPS):
            lo = g * DG
            v = _gelu(p_scr[rows, 3 * D + lo:3 * D + lo + DG])
            mu = jnp.mean(v, axis=-1, keepdims=True)
            d = v - mu
            var = jnp.mean(d * d, axis=-1, keepdims=True)
            vn = d * lax.rsqrt(var + EPS) * slg_ref[:, lo:lo + DG] + slb_ref[:, lo:lo + DG]
            mixed = jnp.dot(wm_scr[g], vn.astype(BF16), preferred_element_type=F32)
            mixed = mixed + sgbt_ref[:, g:g + 1]
            u = _gelu(p_scr[rows, 2 * D + lo:2 * D + lo + DG])
            y_scr[rows, lo:lo + DG] = (u * mixed).astype(BF16)
        return carry
    lax.fori_loop(0, TM // CHUNK, sgu_body, 0)

    yb = jnp.dot(y_scr[...], wso_ref[...], preferred_element_type=F32)
    m = m_scr[...] + jax.nn.sigmoid(p_scr[:, 5 * D:6 * D]) * yb
    o_ref[...] = x_ref[...] + jnp.dot(m.astype(BF16), wo_ref[...], preferred_element_type=F32)


def _mixer(x, gmix, win, cw, cb, clg, clb, wco, slg, slb, sgw, sgbt, wso, wo):
    t = x.shape[0]
    tile = pl.BlockSpec((TM, D), lambda i: (i, 0))
    return pl.pallas_call(
        _mixer_kernel,
        grid=(t // TM,),
        in_specs=[tile, _resident((1, D)), _resident((D, N_IN)), _resident((HALO, D)),
                  _resident((1, D)), _resident((1, D)), _resident((1, D)), _resident((D, D)),
                  _resident((1, D)), _resident((1, D)), _resident((GROUPS, CHUNK, CHUNK)),
                  _resident((CHUNK, GROUPS)), _resident((D, D)), _resident((D, D))],
        out_specs=tile,
        out_shape=jax.ShapeDtypeStruct((t, D), F32),
        scratch_shapes=[
            pltpu.VMEM((TM, D), BF16),
            pltpu.VMEM((TM, N_IN), F32),
            pltpu.VMEM((TM + HALO, D), F32),
            pltpu.VMEM((TM, D), F32),
            pltpu.VMEM((TM, D), BF16),
            pltpu.VMEM((TM, D), BF16),
            pltpu.VMEM((TM, D), F32),
            pltpu.VMEM((GROUPS, CHUNK, CHUNK), BF16),
        ],
        compiler_params=pltpu.CompilerParams(
            dimension_semantics=("arbitrary",), vmem_limit_bytes=VMEM_LIMIT),
        name="mixer",
    )(x, gmix, win, cw, cb, clg, clb, wco, slg, slb, sgw, sgbt, wso, wo)


FC_DENSE = 1408


def _ffn_kernel(x_ref, g_ref, w1_ref, w3_ref, w2_ref, o_ref, h_scr):
    def rms_body(r, carry):
        rows = _rows(r, RB)
        xb = x_ref[rows, :]
        ms = jnp.mean(xb * xb, axis=-1, keepdims=True)
        h_scr[rows, :] = (xb * lax.rsqrt(ms + EPS) * g_ref[...]).astype(BF16)
        return carry
    lax.fori_loop(0, TM // RB, rms_body, 0)

    h = h_scr[...]
    acc = x_ref[...]
    for c in range(FF_DENSE // FC_DENSE):
        lo = c * FC_DENSE
        h1 = jnp.dot(h, w1_ref[:, lo:lo + FC_DENSE], preferred_element_type=F32)
        h3 = jnp.dot(h, w3_ref[:, lo:lo + FC_DENSE], preferred_element_type=F32)
        gate = (h1 * jax.nn.sigmoid(h1) * h3).astype(BF16)
        acc = acc + jnp.dot(gate, w2_ref[lo:lo + FC_DENSE, :], preferred_element_type=F32)
    o_ref[...] = acc


def _ffn(x, g, w1, w3, w2):
    t = x.shape[0]
    tile = pl.BlockSpec((TM, D), lambda i: (i, 0))
    return pl.pallas_call(
        _ffn_kernel,
        grid=(t // TM,),
        in_specs=[tile, _resident((1, D)), _resident((D, FF_DENSE)), _resident((D, FF_DENSE)),
                  _resident((FF_DENSE, D))],
        out_specs=tile,
        out_shape=jax.ShapeDtypeStruct((t, D), F32),
        scratch_shapes=[pltpu.VMEM((TM, D), BF16)],
        compiler_params=pltpu.CompilerParams(
            dimension_semantics=("arbitrary",), vmem_limit_bytes=VMEM_LIMIT),
        name="ffn_dense",
    )(x, g, w1, w3, w2)


def _router_kernel(x_ref, g_ref, r_ref, h_ref, route_ref):
    x = x_ref[...]
    ms = jnp.mean(x * x, axis=-1, keepdims=True)
    h = x * lax.rsqrt(ms + EPS) * g_ref[...]
    h_ref[...] = h
    logits = jnp.dot(h, r_ref[...], preferred_element_type=F32, precision=lax.Precision.HIGHEST)
    lane = lax.broadcasted_iota(jnp.int32, logits.shape, 1)
    neg = jnp.float32(-jnp.inf)
    logits = jnp.where(lane < N_EXPERTS, logits, neg)
    m1 = jnp.max(logits, axis=-1, keepdims=True)
    i1 = jnp.min(jnp.where(logits == m1, lane, LANES), axis=-1, keepdims=True)
    rest = jnp.where(lane == i1, neg, logits)
    m2 = jnp.max(rest, axis=-1, keepdims=True)
    i2 = jnp.min(jnp.where(rest == m2, lane, LANES), axis=-1, keepdims=True)
    e2 = jnp.exp(m2 - m1)
    den = 1.0 + e2
    route = jnp.where(lane == 0, i1.astype(F32), 0.0)
    route = jnp.where(lane == 1, i2.astype(F32), route)
    route = jnp.where(lane == 2, 1.0 / den, route)
    route_ref[...] = jnp.where(lane == 3, e2 / den, route)


RT = 256


def _router(x, g, r_pad):
    t = x.shape[0]
    return pl.pallas_call(
        _router_kernel,
        grid=(t // RT,),
        in_specs=[pl.BlockSpec((RT, D), lambda i: (i, 0)), _resident((1, D)), _resident((D, LANES))],
        out_specs=[pl.BlockSpec((RT, D), lambda i: (i, 0)), pl.BlockSpec((RT, LANES), lambda i: (i, 0))],
        out_shape=[jax.ShapeDtypeStruct((t, D), F32), jax.ShapeDtypeStruct((t, LANES), F32)],
        compiler_params=pltpu.CompilerParams(
            dimension_semantics=("arbitrary",), vmem_limit_bytes=VMEM_LIMIT),
        name="router",
    )(x, g, r_pad)


FC_MOE = 512
NFC = FF_EXPERT // FC_MOE
TM_MOE = 1024


TOP_K = 2
N_TOKENS = 4 * SEQ
R_MAX = TOP_K * N_TOKENS + N_EXPERTS * TM_MOE
N_TILES = R_MAX // TM_MOE


def _route_metadata(route):
    e_flat = route[:, 0:TOP_K].astype(jnp.int32).reshape(-1)
    onehot = (e_flat[:, None] == jnp.arange(N_EXPERTS, dtype=jnp.int32)[None, :]).astype(jnp.int32)
    csum = jnp.cumsum(onehot, axis=0)
    counts = csum[-1]
    padded = ((counts + TM_MOE - 1) // TM_MOE) * TM_MOE
    ends = jnp.cumsum(padded)
    starts = ends - padded
    pos = jnp.sum(onehot * (csum - 1 + starts[None, :]), axis=1).astype(jnp.int32)
    n_used = (ends[-1] // TM_MOE).astype(jnp.int32)
    tile_ids = jnp.arange(N_TILES, dtype=jnp.int32)
    tile_expert = jnp.sum((tile_ids[:, None] * TM_MOE >= ends[None, :]).astype(jnp.int32), axis=1)
    tile_expert = jnp.minimum(tile_expert, N_EXPERTS - 1)
    last = jnp.take(tile_expert, jnp.maximum(n_used - 1, 0))
    tile_expert = jnp.where(tile_ids < n_used, tile_expert, last).astype(jnp.int32)
    return pos, tile_expert, n_used.reshape(1)


TD = 512


def _row_copy(src_ref, src_row, dst_ref, dst_row, sem):
    return pltpu.make_async_copy(src_ref.at[pl.ds(src_row, 1), :], dst_ref.at[pl.ds(dst_row, 1), :], sem)


def _dispatch_kernel(pos_ref, h_ref, xs_init_ref, xs_ref, sem):
    del xs_init_ref
    base = pl.program_id(0) * TD

    def issue(t, carry):
        for k in range(TOP_K):
            _row_copy(h_ref, t, xs_ref, pos_ref[TOP_K * (base + t) + k], sem).start()
        return carry
    lax.fori_loop(0, TD, issue, 0, unroll=8)

    def drain(t, carry):
        for k in range(TOP_K):
            _row_copy(h_ref, t, xs_ref, pos_ref[TOP_K * (base + t) + k], sem).wait()
        return carry
    lax.fori_loop(0, TD, drain, 0, unroll=8)


def _dispatch(pos, h):
    t = h.shape[0]
    xs_init = jnp.zeros((R_MAX, D), F32)
    return pl.pallas_call(
        _dispatch_kernel,
        grid_spec=pltpu.PrefetchScalarGridSpec(
            num_scalar_prefetch=1,
            grid=(t // TD,),
            in_specs=[pl.BlockSpec((TD, D), lambda i, pos: (i, 0)),
                      pl.BlockSpec(memory_space=pl.ANY)],
            out_specs=pl.BlockSpec(memory_space=pl.ANY),
            scratch_shapes=[pltpu.SemaphoreType.DMA(())]),
        out_shape=jax.ShapeDtypeStruct((R_MAX, D), F32),
        input_output_aliases={2: 0},
        compiler_params=pltpu.CompilerParams(dimension_semantics=("arbitrary",)),
        name="moe_dispatch",
    )(pos, h, xs_init)


def _expert_kernel(te_ref, nu_ref, xs_ref, w1_ref, w3_ref, w2_ref, ys_ref, xb_scr):
    i = pl.program_id(0)
    j = pl.program_id(1)

    @pl.when(i < nu_ref[0])
    def _():
        @pl.when(j == 0)
        def _():
            xb_scr[...] = xs_ref[...].astype(BF16)

        h = xb_scr[...]
        h1 = jnp.dot(h, w1_ref[0], preferred_element_type=F32)
        h3 = jnp.dot(h, w3_ref[0], preferred_element_type=F32)
        gate = (h1 * jax.nn.sigmoid(h1) * h3).astype(BF16)
        y = jnp.dot(gate, w2_ref[0], preferred_element_type=F32)

        @pl.when(j == 0)
        def _():
            ys_ref[...] = y

        @pl.when(j > 0)
        def _():
            ys_ref[...] += y

    @pl.when(jnp.logical_and(i >= nu_ref[0], j == 0))
    def _():
        ys_ref[...] = jnp.zeros_like(ys_ref)


def _experts(tile_expert, n_used, xs, w1, w3, w2):
    def row_map(i, j, te, nu):
        return (jnp.minimum(i, nu[0] - 1), 0)

    def col(i, j, nu):
        return jnp.where(i < nu[0], j, NFC - 1)

    return pl.pallas_call(
        _expert_kernel,
        grid_spec=pltpu.PrefetchScalarGridSpec(
            num_scalar_prefetch=2,
            grid=(N_TILES, NFC),
            in_specs=[pl.BlockSpec((TM_MOE, D), row_map),
                      pl.BlockSpec((1, D, FC_MOE), lambda i, j, te, nu: (te[i], 0, col(i, j, nu))),
                      pl.BlockSpec((1, D, FC_MOE), lambda i, j, te, nu: (te[i], 0, col(i, j, nu))),
                      pl.BlockSpec((1, FC_MOE, D), lambda i, j, te, nu: (te[i], col(i, j, nu), 0))],
            out_specs=pl.BlockSpec((TM_MOE, D), lambda i, j, te, nu: (i, 0)),
            scratch_shapes=[pltpu.VMEM((TM_MOE, D), BF16)]),
        out_shape=jax.ShapeDtypeStruct((R_MAX, D), F32),
        compiler_params=pltpu.CompilerParams(
            dimension_semantics=("arbitrary", "arbitrary"), vmem_limit_bytes=VMEM_LIMIT),
        name="moe_experts",
    )(tile_expert, n_used, xs, w1, w3, w2)


def _combine_kernel(pos_ref, x_ref, route_ref, g_ref, ys_ref, o_ref, gath_scr, sem, *, final_norm):
    base = pl.program_id(0) * TD

    def issue(t, carry):
        for k in range(TOP_K):
            _row_copy(ys_ref, pos_ref[TOP_K * (base + t) + k], gath_scr.at[k], t, sem).start()
        return carry
    lax.fori_loop(0, TD, issue, 0, unroll=8)

    def drain(t, carry):
        for k in range(TOP_K):
            _row_copy(ys_ref, pos_ref[TOP_K * (base + t) + k], gath_scr.at[k], t, sem).wait()
        return carry
    lax.fori_loop(0, TD, drain, 0, unroll=8)

    def rows_body(r, carry):
        rows = _rows(r, RB)
        rt = route_ref[rows, :]
        y = x_ref[rows, :] + rt[:, 2:3] * gath_scr[0, rows, :] + rt[:, 3:4] * gath_scr[1, rows, :]
        if final_norm:
            ms = jnp.mean(y * y, axis=-1, keepdims=True)
            y = y * lax.rsqrt(ms + EPS) * g_ref[...]
        o_ref[rows, :] = y
        return carry
    lax.fori_loop(0, TD // RB, rows_body, 0)


def _combine(pos, x, route, g_final, ys, final_norm):
    t = x.shape[0]
    return pl.pallas_call(
        functools.partial(_combine_kernel, final_norm=final_norm),
        grid_spec=pltpu.PrefetchScalarGridSpec(
            num_scalar_prefetch=1,
            grid=(t // TD,),
            in_specs=[pl.BlockSpec((TD, D), lambda i, pos: (i, 0)),
                      pl.BlockSpec((TD, LANES), lambda i, pos: (i, 0)),
                      pl.BlockSpec((1, D), lambda i, pos: (0, 0)),
                      pl.BlockSpec(memory_space=pl.ANY)],
            out_specs=pl.BlockSpec((TD, D), lambda i, pos: (i, 0)),
            scratch_shapes=[pltpu.VMEM((TOP_K, TD, D), F32), pltpu.SemaphoreType.DMA(())]),
        out_shape=jax.ShapeDtypeStruct((t, D), F32),
        compiler_params=pltpu.CompilerParams(dimension_semantics=("arbitrary",)),
        name="moe_combine",
    )(pos, x, route, g_final, ys)


def _final_norm_kernel(x_ref, g_ref, o_ref):
    x = x_ref[...]
    ms = jnp.mean(x * x, axis=-1, keepdims=True)
    o_ref[...] = x * lax.rsqrt(ms + EPS) * g_ref[...]


def _final_norm(x, g):
    t = x.shape[0]
    tile = pl.BlockSpec((TM, D), lambda i: (i, 0))
    return pl.pallas_call(
        _final_norm_kernel,
        grid=(t // TM,),
        in_specs=[tile, _resident((1, D))],
        out_specs=tile,
        out_shape=jax.ShapeDtypeStruct((t, D), F32),
        compiler_params=pltpu.CompilerParams(dimension_semantics=("arbitrary",)),
        name="final_norm",
    )(x, g)


def kernel(x, g_mix, w_in, conv_w, conv_b, conv_ln_g, conv_ln_b, w_conv_out, sg_ln_g, sg_ln_b,
           sg_w, sg_b, w_sg_out, w_o, g_ffn, ffn_w1, ffn_w3, ffn_w2, moe_router, moe_w1, moe_w3,
           moe_w2, g_final):
    bsz, seq, d = x.shape
    assert (bsz * seq, seq, d) == (N_TOKENS, SEQ, D) and SEQ % TM == 0
    xt = x.reshape(bsz * seq, d)
    row = lambda v: v.reshape(1, -1)
    normed = False
    for l in range(DEPTH):
        cw = jnp.pad(conv_w[l], ((0, HALO - CONV_K), (0, 0)))
        xt = _mixer(xt, row(g_mix[l]), w_in[l].astype(BF16), cw, row(conv_b[l]),
                    row(conv_ln_g[l]), row(conv_ln_b[l]), w_conv_out[l].astype(BF16),
                    row(sg_ln_g[l]), row(sg_ln_b[l]), sg_w[l], sg_b[l].T,
                    w_sg_out[l].astype(BF16), w_o[l].astype(BF16))
        j = l // 2
        if l % 2 == 0:
            xt = _ffn(xt, row(g_ffn[l]), ffn_w1[j].astype(BF16), ffn_w3[j].astype(BF16),
                      ffn_w2[j].astype(BF16))
        else:
            r_pad = jnp.pad(moe_router[j], ((0, 0), (0, LANES - N_EXPERTS)))
            h, route = _router(xt, row(g_ffn[l]), r_pad)
            pos, tile_expert, n_used = _route_metadata(route)
            xs = _dispatch(pos, h)
            ys = _experts(tile_expert, n_used, xs, moe_w1[j].astype(BF16), moe_w3[j].astype(BF16),
                          moe_w2[j].astype(BF16))
            normed = l == DEPTH - 1
            xt = _combine(pos, xt, route, row(g_final), ys, normed)
    out = xt if normed else _final_norm(xt, row(g_final))
    return out.reshape(bsz, seq, d)
```
